```python
import math
import jax
import jax.numpy as jnp
from jax import lax
import numpy as np

D_MODEL = 1024
BATCH = 8
SEQ = 2048
DEPTH = 2
DEC_BATCH = 16
DEC_SEQ = 4096
PAST_LEN = 128

HEAD_DIM = 64
DIL_HEADS = D_MODEL // (2 * HEAD_DIM)
DIL_GROUPS = ((128, 1), (512, 4), (2048, 16))
DIFF_HEADS = D_MODEL // (4 * HEAD_DIM)
DIFF_V_DIM = 2 * HEAD_DIM
NA_HEADS = D_MODEL // (2 * HEAD_DIM)
GRID_W = 64
NA_MAX_ROWS = 8
NA_COLS = 16
MLA_HEADS = D_MODEL // (2 * HEAD_DIM)
MLA_Q_LORA = 256
MLA_KV_LORA = 128
MLA_NOPE = 64
MLA_ROPE = 32
MLA_QK = MLA_NOPE + MLA_ROPE
MLA_V = 64
N_EXPERTS = 256
TOP_K = 8
N_GROUPS = 8
TOPK_GROUPS = 4
MOE_D_FF = 256
SHARED_D_FF = 256
ROUTED_SCALE = 2.5
EXPERT_BLOCK = 128
ROPE_THETA = 10000.0
EPS = 1e-6
Q_BLOCK = 128
NEG = -1e30
N_EVEN = (DEPTH + 1) // 2
N_ODD = DEPTH // 2
EVEN_SIZES = (DIL_HEADS * HEAD_DIM,) * 3 + (DIFF_HEADS * HEAD_DIM,) * 4 + (DIFF_HEADS * DIFF_V_DIM,)
ODD_SIZES = (NA_HEADS * HEAD_DIM,) * 3 + (MLA_Q_LORA, MLA_KV_LORA, MLA_ROPE)
EVEN_IN = sum(EVEN_SIZES)
ODD_IN = sum(ODD_SIZES)
EVEN_MIX = DIL_HEADS * HEAD_DIM + DIFF_HEADS * DIFF_V_DIM
ODD_MIX = NA_HEADS * HEAD_DIM + MLA_HEADS * MLA_V

kernel_name = 'hybrid_dilated_diff_natten_mla_moe_encoder'


def _split(x, sizes):
    return jnp.split(x, np.cumsum(sizes)[:-1].tolist(), axis=-1)


def _rms_norm(x, g):
    xf = x.astype(jnp.float32)
    y = xf * lax.rsqrt(jnp.mean(xf * xf, axis=-1, keepdims=True) + EPS)
    return (y * g.astype(jnp.float32)).astype(x.dtype)


def _rope(x, pos):
    half = x.shape[-1] // 2
    inv = ROPE_THETA ** (-jnp.arange(half, dtype=jnp.float32) * 2.0 / x.shape[-1])
    ang = pos[:, None] * inv[None, :]
    cos = jnp.cos(ang)[None, :, None, :]
    sin = jnp.sin(ang)[None, :, None, :]
    x1 = x[..., :half].astype(jnp.float32)
    x2 = x[..., half:].astype(jnp.float32)
    return jnp.concatenate([x1 * cos - x2 * sin, x2 * cos + x1 * sin], axis=-1).astype(x.dtype)


def _map_query_blocks(fn, qs):
    n, s = qs[0].shape[:2]
    nb = s // Q_BLOCK
    xs = tuple(jnp.moveaxis(q.reshape((n, nb, Q_BLOCK) + q.shape[2:]), 1, 0) for q in qs)
    out = lax.map(lambda blk: fn(*blk), xs)
    out = jnp.moveaxis(out, 0, 1)
    return out.reshape((n, s) + out.shape[3:])


def _banded_attention(q, k, v, n_side):
    m_, l, nh, hd = q.shape
    qb_size = math.gcd(l, 128)
    nb = l // qb_size
    kw = qb_size + 2 * n_side
    pad = ((0, 0), (n_side, n_side), (0, 0), (0, 0))
    kp = jnp.pad(k, pad)
    vp = jnp.pad(v, pad)
    idx = (jnp.arange(nb) * qb_size)[:, None] + jnp.arange(kw)[None, :]
    kb = kp[:, idx]
    vb = vp[:, idx]
    qb = q.reshape(m_, nb, qb_size, nh, hd)
    sc = jnp.einsum('mbqhd,mbkhd->mbhqk', qb, kb).astype(jnp.float32) * (hd ** -0.5)
    key_pos = idx - n_side
    rel = jnp.arange(kw)[None, :] - n_side - jnp.arange(qb_size)[:, None]
    ok = (jnp.abs(rel) <= n_side)[None] & ((key_pos >= 0) & (key_pos < l))[:, None, :]
    sc = jnp.where(ok[None, :, None], sc, NEG)
    mx = jnp.max(sc, axis=-1, keepdims=True)
    e = jnp.exp(sc - mx)
    den = jnp.sum(e, axis=-1, keepdims=True)
    o = jnp.einsum('mbhqk,mbkhd->mbqhd', (e / den).astype(v.dtype), vb)
    lse = (mx + jnp.log(den))[..., 0]
    return o.reshape(m_, l, nh, hd), jnp.moveaxis(lse, 2, 3).reshape(m_, l, nh)


def _dilated_attention(q, k, v):
    n, s, nh, hd = q.shape
    outs, lses = [], []
    for window, dil in DIL_GROUPS:
        n_side = (window // 2) // dil
        l = s // dil
        def to_sub(t):
            return t.reshape(n, l, dil, nh, hd).transpose(0, 2, 1, 3, 4).reshape(n * dil, l, nh, hd)
        o, lse = _banded_attention(to_sub(q), to_sub(k), to_sub(v), n_side)
        outs.append(o.reshape(n, dil, l, nh, hd).transpose(0, 2, 1, 3, 4).reshape(n, s, nh, hd))
        lses.append(lse.reshape(n, dil, l, nh).transpose(0, 2, 1, 3).reshape(n, s, nh))
    wts = jax.nn.softmax(jnp.stack(lses), axis=0)
    o = jnp.einsum('gnsh,gnshd->nshd', wts, jnp.stack(outs).astype(jnp.float32))
    return o.astype(q.dtype)


def _differential_attention(q1, q2, k1, k2, v, lam):
    scale = q1.shape[-1] ** -0.5
    def blk(q1b, q2b):
        s1 = jnp.einsum('nqhd,nkhd->nhqk', q1b, k1).astype(jnp.float32) * scale
        s2 = jnp.einsum('nqhd,nkhd->nhqk', q2b, k2).astype(jnp.float32) * scale
        a = jax.nn.softmax(s1, axis=-1) - lam * jax.nn.softmax(s2, axis=-1)
        return jnp.einsum('nhqk,nkhd->nqhd', a.astype(v.dtype), v)
    return _map_query_blocks(blk, (q1, q2))


def _dense_attention(q, k, v):
    scale = q.shape[-1] ** -0.5
    def blk(qb):
        sc = jnp.einsum('nqhd,nkhd->nhqk', qb, k).astype(jnp.float32) * scale
        p = jax.nn.softmax(sc, axis=-1)
        return jnp.einsum('nhqk,nkhd->nqhd', p.astype(v.dtype), v)
    return _map_query_blocks(blk, (q,))


def _neighborhood_attention(q, k, v, rpb):
    n, s, nh, hd = q.shape
    rows = s // GRID_W
    kh = min(NA_MAX_ROWS, rows)
    kw = NA_COLS
    qg = q.reshape(n, rows, GRID_W, nh, hd)
    kg = k.reshape(n, rows, GRID_W, nh, hd)
    vg = v.reshape(n, rows, GRID_W, nh, hd)
    row_start = jnp.clip(jnp.arange(rows) - kh // 2, 0, rows - kh)
    cols = jnp.arange(GRID_W)
    col_start = jnp.clip(cols - kw // 2, 0, GRID_W - kw)
    col_ok = (cols[None, :] >= col_start[:, None]) & (cols[None, :] < col_start[:, None] + kw)
    dc = jnp.clip(cols[None, :] - cols[:, None], -(kw - 1), kw - 1) + kw - 1
    scale = hd ** -0.5
    def row_fn(args):
        r, qr = args
        rs = row_start[r]
        kr = lax.dynamic_slice_in_dim(kg, rs, kh, axis=1)
        vr = lax.dynamic_slice_in_dim(vg, rs, kh, axis=1)
        dr = rs + jnp.arange(kh) - r + NA_MAX_ROWS - 1
        bias = rpb[:, dr[None, :, None], dc[:, None, :]]
        sc = jnp.einsum('nchd,nrkhd->nhcrk', qr, kr).astype(jnp.float32) * scale + bias[None].astype(jnp.float32)
        sc = jnp.where(col_ok[None, None, :, None, :], sc, NEG)
        p = jax.nn.softmax(sc.reshape(n, nh, GRID_W, kh * GRID_W), axis=-1).reshape(sc.shape)
        return jnp.einsum('nhcrk,nrkhd->nchd', p.astype(v.dtype), vr)
    out = lax.map(row_fn, (jnp.arange(rows), jnp.moveaxis(qg, 1, 0)))
    return jnp.moveaxis(out, 0, 1).reshape(n, s, nh, hd)


def _route(x, w_router, router_bias):
    t = x.shape[0]
    scores = jax.nn.sigmoid((x @ w_router).astype(jnp.float32))
    biased = scores + router_bias.astype(jnp.float32)
    grp = lax.top_k(biased.reshape(t, N_GROUPS, N_EXPERTS // N_GROUPS), 2)[0].sum(-1)
    _, gidx = lax.top_k(grp, TOPK_GROUPS)
    gmask = jax.nn.one_hot(gidx, N_GROUPS, dtype=jnp.float32).sum(1) > 0
    emask = jnp.repeat(gmask, N_EXPERTS // N_GROUPS, axis=1)
    _, eidx = lax.top_k(jnp.where(emask, biased, -jnp.inf), TOP_K)
    w = jnp.take_along_axis(scores, eidx, axis=1)
    w = w / jnp.sum(w, axis=-1, keepdims=True) * ROUTED_SCALE
    return eidx, w


def _moe(h, w_router, router_bias, w_gate, w_up, w_down, ws_gate, ws_up, ws_down):
    n, s, d = h.shape
    t = n * s
    x = h.reshape(t, d)
    eidx, ew = _route(x, w_router, router_bias)
    a = t * TOP_K
    flat_e = eidx.reshape(a)
    flat_tok = jnp.arange(a, dtype=jnp.int32) // TOP_K
    flat_w = ew.reshape(a).astype(x.dtype)
    order = jnp.argsort(flat_e)
    sorted_e = flat_e[order]
    counts = jnp.zeros((N_EXPERTS,), jnp.int32).at[flat_e].add(1)
    padded = (counts + EXPERT_BLOCK - 1) // EXPERT_BLOCK * EXPERT_BLOCK
    start = jnp.cumsum(counts) - counts
    pend = jnp.cumsum(padded)
    pstart = pend - padded
    dest = pstart[sorted_e] + jnp.arange(a, dtype=jnp.int32) - start[sorted_e]
    n_blocks = -(-a // EXPERT_BLOCK) + N_EXPERTS
    p = n_blocks * EXPERT_BLOCK
    buf_tok = jnp.full((p,), t, jnp.int32).at[dest].set(flat_tok[order])
    buf_w = jnp.zeros((p,), x.dtype).at[dest].set(flat_w[order])
    block_e = jnp.minimum(jnp.searchsorted(pend, jnp.arange(n_blocks, dtype=jnp.int32) * EXPERT_BLOCK, side='right'), N_EXPERTS - 1)
    x_pad = jnp.concatenate([x, jnp.zeros((1, d), x.dtype)], axis=0)
    def block_fn(args):
        tok, wb, e = args
        xb = x_pad[tok]
        hb = jax.nn.silu(xb @ w_gate[e]) * (xb @ w_up[e])
        return (hb @ w_down[e]) * wb[:, None]
    out = lax.map(block_fn, (buf_tok.reshape(n_blocks, EXPERT_BLOCK), buf_w.reshape(n_blocks, EXPERT_BLOCK), block_e))
    routed = jnp.zeros((t + 1, d), x.dtype).at[buf_tok].add(out.reshape(p, d))[:t]
    shared = (jax.nn.silu(x @ ws_gate) * (x @ ws_up)) @ ws_down
    return (routed + shared).reshape(n, s, d)


def _even_mixer(h, pos, p, j, layer):
    n, s, _ = h.shape
    qa, ka, va, q1, q2, k1, k2, vb = _split(h @ p['even_w_in'][j], EVEN_SIZES)
    def heads(t, nh):
        return t.reshape(n, s, nh, -1)
    qa = _rope(_rms_norm(heads(qa, DIL_HEADS), p['dil_q_norm'][j]), pos)
    ka = _rope(_rms_norm(heads(ka, DIL_HEADS), p['dil_k_norm'][j]), pos)
    oa = _dilated_attention(qa, ka, heads(va, DIL_HEADS))
    gq = p['diff_q_norm'][j]
    gk = p['diff_k_norm'][j]
    q1 = _rope(_rms_norm(heads(q1, DIFF_HEADS), gq), pos)
    q2 = _rope(_rms_norm(heads(q2, DIFF_HEADS), gq), pos)
    k1 = _rope(_rms_norm(heads(k1, DIFF_HEADS), gk), pos)
    k2 = _rope(_rms_norm(heads(k2, DIFF_HEADS), gk), pos)
    lam_init = 0.8 - 0.6 * math.exp(-0.3 * layer)
    f32 = jnp.float32
    lam = (jnp.exp(jnp.sum(p['diff_lambda_q1'][j].astype(f32) * p['diff_lambda_k1'][j].astype(f32)))
           - jnp.exp(jnp.sum(p['diff_lambda_q2'][j].astype(f32) * p['diff_lambda_k2'][j].astype(f32))) + lam_init)
    ob = _differential_attention(q1, q2, k1, k2, heads(vb, DIFF_HEADS), lam)
    ob = _rms_norm(ob, p['diff_subln'][j]) * (1.0 - lam_init)
    mixed = jnp.concatenate([oa.reshape(n, s, -1), ob.reshape(n, s, -1)], axis=-1)
    return mixed @ p['even_w_out'][j]


def _odd_mixer(h, pos, p, j):
    n, s, _ = h.shape
    qc, kc, vc, cq, ckv, kr = _split(h @ p['odd_w_in'][j], ODD_SIZES)
    qc = _rms_norm(qc.reshape(n, s, NA_HEADS, HEAD_DIM), p['na_q_norm'][j])
    kc = _rms_norm(kc.reshape(n, s, NA_HEADS, HEAD_DIM), p['na_k_norm'][j])
    oc = _neighborhood_attention(qc, kc, vc.reshape(n, s, NA_HEADS, HEAD_DIM), p['na_rpb'][j])
    q = (_rms_norm(cq, p['mla_cq_norm'][j]) @ p['mla_w_q_up'][j]).reshape(n, s, MLA_HEADS, MLA_QK)
    kv = (_rms_norm(ckv, p['mla_ckv_norm'][j]) @ p['mla_w_kv_up'][j]).reshape(n, s, MLA_HEADS, MLA_NOPE + MLA_V)
    k_nope = kv[..., :MLA_NOPE]
    vd = kv[..., MLA_NOPE:]
    k = jnp.concatenate([k_nope, jnp.broadcast_to(kr[:, :, None, :], (n, s, MLA_HEADS, MLA_ROPE))], axis=-1)
    q = _rms_norm(q, p['mla_q_norm'][j])
    k = _rms_norm(k, p['mla_k_norm'][j])
    q = jnp.concatenate([q[..., :MLA_NOPE], _rope(q[..., MLA_NOPE:], pos)], axis=-1)
    k = jnp.concatenate([k[..., :MLA_NOPE], _rope(k[..., MLA_NOPE:], pos)], axis=-1)
    od = _dense_attention(q, k, vd)
    mixed = jnp.concatenate([oc.reshape(n, s, -1), od.reshape(n, s, -1)], axis=-1)
    return mixed @ p['odd_w_out'][j]


def _trunk(x, c, p):
    n, s, d = x.shape
    pos = jnp.arange(s, dtype=jnp.float32)
    for i in range(DEPTH):
        mod = jax.nn.silu(c) @ p['ada_w'][i] + p['ada_b'][i]
        sh_a, sc_a, g_a, sh_f, sc_f, g_f = [m[:, None, :] for m in jnp.split(mod, 6, axis=-1)]
        h = _rms_norm(x, p['mix_norm'][i]) * (1.0 + sc_a) + sh_a
        if i % 2 == 0:
            mixed = _even_mixer(h, pos, p, i // 2, i)
        else:
            mixed = _odd_mixer(h, pos, p, i // 2)
        x = x + g_a * mixed
        h = _rms_norm(x, p['ffn_norm'][i]) * (1.0 + sc_f) + sh_f
        x = x + g_f * _moe(h, p['router_w'][i], p['router_bias'][i], p['expert_w_gate'][i], p['expert_w_up'][i],
                           p['expert_w_down'][i], p['shared_w_gate'][i], p['shared_w_up'][i], p['shared_w_down'][i])
    return x


def setup_inputs(seed: int = 0) -> dict:
    key = jax.random.key(seed)
    keys = iter(jax.random.split(key, 40))
    def nrm(shape, scale):
        return jax.random.normal(next(keys), shape, jnp.float32) * scale
    def gain(shape):
        return 1.0 + nrm(shape, 0.02)
    d = D_MODEL
    return {
        'x_prompt': nrm((BATCH, SEQ, d), 1.0),
        'x_sample': nrm((DEC_BATCH, DEC_SEQ, d), 1.0),
        'c_prompt': nrm((BATCH, d), 1.0),
        'c_sample': nrm((DEC_BATCH, d), 1.0),
        'ada_w': nrm((DEPTH, d, 6 * d), 0.5 * d ** -0.5),
        'ada_b': nrm((DEPTH, 6 * d), 0.02),
        'mix_norm': gain((DEPTH, d)),
        'ffn_norm': gain((DEPTH, d)),
        'even_w_in': nrm((N_EVEN, d, EVEN_IN), d ** -0.5),
        'even_w_out': nrm((N_EVEN, EVEN_MIX, d), EVEN_MIX ** -0.5),
        'dil_q_norm': gain((N_EVEN, HEAD_DIM)),
        'dil_k_norm': gain((N_EVEN, HEAD_DIM)),
        'diff_q_norm': gain((N_EVEN, HEAD_DIM)),
        'diff_k_norm': gain((N_EVEN, HEAD_DIM)),
        'diff_lambda_q1': nrm((N_EVEN, HEAD_DIM), 0.1),
        'diff_lambda_k1': nrm((N_EVEN, HEAD_DIM), 0.1),
        'diff_lambda_q2': nrm((N_EVEN, HEAD_DIM), 0.1),
        'diff_lambda_k2': nrm((N_EVEN, HEAD_DIM), 0.1),
        'diff_subln': gain((N_EVEN, DIFF_V_DIM)),
        'odd_w_in': nrm((N_ODD, d, ODD_IN), d ** -0.5),
        'odd_w_out': nrm((N_ODD, ODD_MIX, d), ODD_MIX ** -0.5),
        'na_q_norm': gain((N_ODD, HEAD_DIM)),
        'na_k_norm': gain((N_ODD, HEAD_DIM)),
        'na_rpb': nrm((N_ODD, NA_HEADS, 2 * NA_MAX_ROWS - 1, 2 * NA_COLS - 1), 0.1),
        'mla_cq_norm': gain((N_ODD, MLA_Q_LORA)),
        'mla_ckv_norm': gain((N_ODD, MLA_KV_LORA)),
        'mla_w_q_up': nrm((N_ODD, MLA_Q_LORA, MLA_HEADS * MLA_QK), MLA_Q_LORA ** -0.5),
        'mla_w_kv_up': nrm((N_ODD, MLA_KV_LORA, MLA_HEADS * (MLA_NOPE + MLA_V)), MLA_KV_LORA ** -0.5),
        'mla_q_norm': gain((N_ODD, MLA_QK)),
        'mla_k_norm': gain((N_ODD, MLA_QK)),
        'router_w': nrm((DEPTH, d, N_EXPERTS), d ** -0.5),
        'router_bias': nrm((DEPTH, N_EXPERTS), 0.01),
        'expert_w_gate': nrm((DEPTH, N_EXPERTS, d, MOE_D_FF), d ** -0.5),
        'expert_w_up': nrm((DEPTH, N_EXPERTS, d, MOE_D_FF), d ** -0.5),
        'expert_w_down': nrm((DEPTH, N_EXPERTS, MOE_D_FF, d), MOE_D_FF ** -0.5),
        'shared_w_gate': nrm((DEPTH, d, SHARED_D_FF), d ** -0.5),
        'shared_w_up': nrm((DEPTH, d, SHARED_D_FF), d ** -0.5),
        'shared_w_down': nrm((DEPTH, SHARED_D_FF, d), SHARED_D_FF ** -0.5),
    }


def reference(x_prompt, x_sample, c_prompt, c_sample, ada_w, ada_b, mix_norm, ffn_norm,
              even_w_in, even_w_out, dil_q_norm, dil_k_norm, diff_q_norm, diff_k_norm,
              diff_lambda_q1, diff_lambda_k1, diff_lambda_q2, diff_lambda_k2, diff_subln,
              odd_w_in, odd_w_out, na_q_norm, na_k_norm, na_rpb, mla_cq_norm, mla_ckv_norm,
              mla_w_q_up, mla_w_kv_up, mla_q_norm, mla_k_norm, router_w, router_bias,
              expert_w_gate, expert_w_up, expert_w_down, shared_w_gate, shared_w_up, shared_w_down):
    params = dict(
        ada_w=ada_w, ada_b=ada_b, mix_norm=mix_norm, ffn_norm=ffn_norm,
        even_w_in=even_w_in, even_w_out=even_w_out, dil_q_norm=dil_q_norm, dil_k_norm=dil_k_norm,
        diff_q_norm=diff_q_norm, diff_k_norm=diff_k_norm, diff_lambda_q1=diff_lambda_q1,
        diff_lambda_k1=diff_lambda_k1, diff_lambda_q2=diff_lambda_q2, diff_lambda_k2=diff_lambda_k2,
        diff_subln=diff_subln, odd_w_in=odd_w_in, odd_w_out=odd_w_out, na_q_norm=na_q_norm,
        na_k_norm=na_k_norm, na_rpb=na_rpb, mla_cq_norm=mla_cq_norm, mla_ckv_norm=mla_ckv_norm,
        mla_w_q_up=mla_w_q_up, mla_w_kv_up=mla_w_kv_up, mla_q_norm=mla_q_norm, mla_k_norm=mla_k_norm,
        router_w=router_w, router_bias=router_bias, expert_w_gate=expert_w_gate, expert_w_up=expert_w_up,
        expert_w_down=expert_w_down, shared_w_gate=shared_w_gate, shared_w_up=shared_w_up,
        shared_w_down=shared_w_down)
    y_prompt = _trunk(x_prompt, c_prompt, params)
    y_sample = _trunk(x_sample, c_sample, params)
    return (y_prompt, y_sample)
```

```python
import functools
import math

import numpy as np
import jax
import jax.numpy as jnp
from jax import lax
from jax.experimental import pallas as pl
from jax.experimental.pallas import tpu as pltpu

D_MODEL = 1024
DEPTH = 2
HEAD_DIM = 64
DIL_HEADS = 8
DIL_GROUPS = ((128, 1), (512, 4), (2048, 16))
DIFF_HEADS = 4
DIFF_V_DIM = 128
NA_HEADS = 8
GRID_W = 64
NA_MAX_ROWS = 8
NA_COLS = 16
MLA_HEADS = 8
MLA_Q_LORA = 256
MLA_KV_LORA = 128
MLA_NOPE = 64
MLA_ROPE = 32
MLA_QK = MLA_NOPE + MLA_ROPE
MLA_V = 64
N_EXPERTS = 256
TOP_K = 8
N_GROUPS = 8
TOPK_GROUPS = 4
GROUP_SIZE = N_EXPERTS // N_GROUPS
MOE_D_FF = 256
ROUTED_SCALE = 2.5
ROPE_THETA = 10000.0
EPS = 1e-6
NEG = -1e30

LANES = 128
VMEM_LIMIT = 56 * 1024 * 1024

EXPERT_ROWS = 256
TM_PROJ = 256
TQ_DENSE = 256
TQ_BAND = 128
TM_ROUTE = 256
TM_DISPATCH = 256
TM_COMBINE = 128

F32 = jnp.float32
BF16 = jnp.bfloat16


def _params(sem, vmem=None):
    return pltpu.CompilerParams(dimension_semantics=sem, vmem_limit_bytes=vmem or VMEM_LIMIT)


def _dot(a, b):
    return jnp.dot(a, b, preferred_element_type=F32)


def _dot_nt(a, b):
    return lax.dot_general(a, b, (((1,), (1,)), ((), ())), preferred_element_type=F32)


def _silu(x):
    return x * (1.0 / (1.0 + jnp.exp(-x)))


def _norm_mod(x, g, sc, sh):
    ms = jnp.mean(x * x, axis=-1, keepdims=True)
    return (x * lax.rsqrt(ms + EPS) * g) * (1.0 + sc) + sh


def _ada_kernel(c_ref, w_ref, b_ref, o_ref):
    a = _silu(c_ref[...]).astype(BF16)
    o_ref[...] = _dot(a, w_ref[...].astype(BF16)) + b_ref[...]


def _ada_mod(c_all, ada_w, ada_b):
    nb, d = c_all.shape
    depth, _, six_d = ada_w.shape
    tn = 1536
    return pl.pallas_call(
        _ada_kernel,
        out_shape=jax.ShapeDtypeStruct((depth, nb, six_d), F32),
        grid=(depth, six_d // tn),
        in_specs=[
            pl.BlockSpec((nb, d), lambda i, j: (0, 0)),
            pl.BlockSpec((None, d, tn), lambda i, j: (i, 0, j)),
            pl.BlockSpec((None, 1, tn), lambda i, j: (i, 0, j)),
        ],
        out_specs=pl.BlockSpec((None, nb, tn), lambda i, j: (i, 0, j)),
        compiler_params=_params(("arbitrary", "arbitrary")),
        name="ada_mod",
    )(c_all, ada_w, ada_b.reshape(depth, 1, six_d))


def _head_norm(y, bd, gain, inv_dim):
    ss = _dot((y * y).astype(BF16), bd)
    return y * lax.rsqrt(ss * inv_dim + EPS) * gain


def _rope(yn, prot, cos, sin):
    return yn * cos + _dot(yn.astype(BF16), prot) * sin


def _even_in_kernel(x_ref, g_ref, sc_ref, sh_ref, w_ref, cos_ref, sin_ref, bd_ref, prot_ref,
                    gqa_ref, gka_ref, gqb_ref, gkb_ref,
                    qa_ref, ka_ref, va_ref, qb_ref, kb_ref, vb_ref):
    h = _norm_mod(x_ref[...], g_ref[...], sc_ref[...], sh_ref[...])
    y = _dot(h.astype(BF16), w_ref[...])
    cos = cos_ref[...]
    sin = sin_ref[...]
    bd = bd_ref[...]
    prot = prot_ref[...]
    scale = HEAD_DIM ** -0.5

    def qk(col, gain, s):
        yc = y[:, col:col + LANES]
        return (_rope(_head_norm(yc, bd, gain, 1.0 / HEAD_DIM), prot, cos, sin) * s).astype(BF16)

    for c in range(4):
        o = c * LANES
        qa_ref[:, o:o + LANES] = qk(o, gqa_ref[...], scale)
        ka_ref[:, o:o + LANES] = qk(512 + o, gka_ref[...], 1.0)
        va_ref[:, o:o + LANES] = y[:, 1024 + o:1024 + o + LANES].astype(BF16)
        qb_ref[:, o:o + LANES] = qk(1536 + o, gqb_ref[...], scale)
        kb_ref[:, o:o + LANES] = qk(2048 + o, gkb_ref[...], 1.0)
        vb_ref[:, o:o + LANES] = y[:, 2560 + o:2560 + o + LANES].astype(BF16)


def _even_in_proj(x, gain, sc, sh, w, cos, sin, bd, prot, gqa, gka, gqb, gkb):
    n, s, d = x.shape
    tm = TM_PROJ
    tok = pl.BlockSpec((None, tm, d), lambda b, i: (b, i, 0))
    per_b = pl.BlockSpec((None, 1, d), lambda b, i: (b, 0, 0))
    const = lambda shape: pl.BlockSpec(shape, lambda b, i: (0,) * len(shape))
    half = pl.BlockSpec((None, tm, 512), lambda b, i: (b, i, 0))
    out = jax.ShapeDtypeStruct((n, s, 512), BF16)
    return pl.pallas_call(
        _even_in_kernel,
        out_shape=(out,) * 6,
        grid=(n, s // tm),
        in_specs=[tok, const((1, d)), per_b, per_b, const(w.shape),
                  pl.BlockSpec((tm, LANES), lambda b, i: (i, 0)),
                  pl.BlockSpec((tm, LANES), lambda b, i: (i, 0)),
                  const((LANES, LANES)), const((LANES, LANES)),
                  const((1, LANES)), const((1, LANES)), const((1, LANES)), const((1, LANES))],
        out_specs=(half,) * 6,
        compiler_params=_params(("arbitrary", "arbitrary")),
        name="even_in_proj",
    )(x, gain, sc, sh, w, cos, sin, bd, prot, gqa, gka, gqb, gkb)


def _odd_in_kernel(x_ref, g_ref, sc_ref, sh_ref, w_ref, cos_ref, sin_ref, bd_ref, prot_ref,
                   gqc_ref, gkc_ref, gcq_ref, gckv_ref, wq_ref, wk_ref, wv_ref, gq_ref, gk_ref,
                   qc_ref, kc_ref, vc_ref, qm_ref, km_ref, vm_ref):
    h = _norm_mod(x_ref[...], g_ref[...], sc_ref[...], sh_ref[...])
    y = _dot(h.astype(BF16), w_ref[...])
    bd = bd_ref[...]
    for c in range(4):
        o = c * LANES
        qc_ref[:, o:o + LANES] = (_head_norm(y[:, o:o + LANES], bd, gqc_ref[...], 1.0 / HEAD_DIM)
                                  * (HEAD_DIM ** -0.5)).astype(BF16)
        kc_ref[:, o:o + LANES] = _head_norm(y[:, 512 + o:512 + o + LANES], bd, gkc_ref[...],
                                            1.0 / HEAD_DIM).astype(BF16)
        vc_ref[:, o:o + LANES] = y[:, 1024 + o:1024 + o + LANES].astype(BF16)

    cq = y[:, 1536:1536 + MLA_Q_LORA]
    cqn = cq * lax.rsqrt(jnp.mean(cq * cq, axis=-1, keepdims=True) + EPS) * gcq_ref[...]
    ckv = y[:, 1792:1792 + MLA_KV_LORA]
    ckvn = ckv * lax.rsqrt(jnp.mean(ckv * ckv, axis=-1, keepdims=True) + EPS) * gckv_ref[...]
    kr = y[:, 1920:2048]
    q = _dot(cqn.astype(BF16), wq_ref[...])
    kn = _dot(ckvn.astype(BF16), wk_ref[...])
    vm_ref[...] = _dot(ckvn.astype(BF16), wv_ref[...]).astype(BF16)
    cos = cos_ref[...]
    sin = sin_ref[...]
    prot = prot_ref[...]

    def head_norm96(t, gain):
        ss = jnp.sum(t * t, axis=-1, keepdims=True)
        return t * lax.rsqrt(ss * (1.0 / MLA_QK) + EPS) * gain

    for hd in range(MLA_HEADS):
        o = hd * LANES
        qh = head_norm96(q[:, o:o + LANES], gq_ref[...])
        qm_ref[:, o:o + LANES] = (_rope(qh, prot, cos, sin) * (MLA_QK ** -0.5)).astype(BF16)
        kh = head_norm96(kn[:, o:o + LANES] + kr, gk_ref[...])
        km_ref[:, o:o + LANES] = _rope(kh, prot, cos, sin).astype(BF16)


def _odd_in_proj(x, gain, sc, sh, w, cos, sin, bd, prot, gqc, gkc, gcq, gckv, wq, wk, wv, gq, gk):
    n, s, d = x.shape
    tm = TM_PROJ
    tok = pl.BlockSpec((None, tm, d), lambda b, i: (b, i, 0))
    per_b = pl.BlockSpec((None, 1, d), lambda b, i: (b, 0, 0))
    const = lambda shape: pl.BlockSpec(shape, lambda b, i: (0,) * len(shape))
    half = pl.BlockSpec((None, tm, 512), lambda b, i: (b, i, 0))
    full = pl.BlockSpec((None, tm, 1024), lambda b, i: (b, i, 0))
    o512 = jax.ShapeDtypeStruct((n, s, 512), BF16)
    o1024 = jax.ShapeDtypeStruct((n, s, 1024), BF16)
    return pl.pallas_call(
        _odd_in_kernel,
        out_shape=(o512, o512, o512, o1024, o1024, o512),
        grid=(n, s // tm),
        in_specs=[tok, const((1, d)), per_b, per_b, const(w.shape),
                  pl.BlockSpec((tm, LANES), lambda b, i: (i, 0)),
                  pl.BlockSpec((tm, LANES), lambda b, i: (i, 0)),
                  const((LANES, LANES)), const((LANES, LANES)),
                  const((1, LANES)), const((1, LANES)), const((1, MLA_Q_LORA)), const((1, MLA_KV_LORA)),
                  const(wq.shape), const(wk.shape), const(wv.shape),
                  const((1, LANES)), const((1, LANES))],
        out_specs=(half, half, half, full, full, half),
        compiler_params=_params(("arbitrary", "arbitrary")),
        name="odd_in_proj",
    )(x, gain, sc, sh, w, cos, sin, bd, prot, gqc, gkc, gcq, gckv, wq, wk, wv, gq, gk)


def _softmax_pv(s, v):
    m = jnp.max(s, axis=-1, keepdims=True)
    e = jnp.exp(s - m)
    den = jnp.sum(e, axis=-1, keepdims=True)
    return _dot(e.astype(BF16), v) * (1.0 / den), m + jnp.log(den)


def _banded_kernel(q_ref, kp_ref, kc_ref, kn_ref, vp_ref, vc_ref, vn_ref, o_ref, l_ref, *, tq, side, length):
    lb = pl.program_id(2)
    q = q_ref[...]
    k = jnp.concatenate([kp_ref[tq - side:, :], kc_ref[...], kn_ref[:side, :]], axis=0)
    v = jnp.concatenate([vp_ref[tq - side:, :], vc_ref[...], vn_ref[:side, :]], axis=0)
    kw = tq + 2 * side
    qpos = lax.broadcasted_iota(jnp.int32, (tq, kw), 0)
    kpos = lax.broadcasted_iota(jnp.int32, (tq, kw), 1) - side
    kabs = kpos + lb * tq
    ok = (jnp.abs(kpos - qpos) <= side) & (kabs >= 0) & (kabs < length)
    outs, lses = [], []
    for h in range(DIL_HEADS):
        c = slice(h * HEAD_DIM, (h + 1) * HEAD_DIM)
        s = jnp.where(ok, _dot_nt(q[:, c], k[:, c]), NEG)
        o, lse = _softmax_pv(s, v[:, c])
        outs.append(o)
        lses.append(jnp.broadcast_to(lse, (tq, HEAD_DIM)))
    o_ref[...] = jnp.concatenate(outs, axis=-1).astype(BF16)
    l_ref[...] = jnp.concatenate(lses, axis=-1)


def _banded_attention(q, k, v, dil, side):
    n, s, c = q.shape
    length = s // dil
    tq = TQ_BAND
    nb = length // tq
    qv, kv, vv = (t.reshape(n, length, dil * c) for t in (q, k, v))
    cur = pl.BlockSpec((None, tq, c), lambda b, r, i: (b, i, r))
    prev = pl.BlockSpec((None, tq, c), lambda b, r, i: (b, jnp.maximum(i - 1, 0), r))
    nxt = pl.BlockSpec((None, tq, c), lambda b, r, i: (b, jnp.minimum(i + 1, nb - 1), r))
    o, lse = pl.pallas_call(
        functools.partial(_banded_kernel, tq=tq, side=side, length=length),
        out_shape=(jax.ShapeDtypeStruct((n, length, dil * c), BF16),
                   jax.ShapeDtypeStruct((n, length, dil * c), F32)),
        grid=(n, dil, nb),
        in_specs=[cur, prev, cur, nxt, prev, cur, nxt],
        out_specs=(cur, cur),
        compiler_params=_params(("arbitrary",) * 3),
        name=f"banded_attn_d{dil}",
    )(qv, kv, kv, kv, vv, vv, vv)
    return o.reshape(n, s, c), lse.reshape(n, s, c)


def _diff_kernel(lq1_ref, lk1_ref, lq2_ref, lk2_ref, subln_ref, q_ref, k_ref, v_ref, o_ref, *, lam_init):
    lam = (jnp.exp(jnp.sum(lq1_ref[...] * lk1_ref[...], axis=-1, keepdims=True))
           - jnp.exp(jnp.sum(lq2_ref[...] * lk2_ref[...], axis=-1, keepdims=True)) + lam_init)
    q = q_ref[...]
    half = DIFF_HEADS * HEAD_DIM
    for h in range(DIFF_HEADS):
        c1 = slice(h * HEAD_DIM, (h + 1) * HEAD_DIM)
        c2 = slice(half + h * HEAD_DIM, half + (h + 1) * HEAD_DIM)
        vh = v_ref[:, h * DIFF_V_DIM:(h + 1) * DIFF_V_DIM]
        o1, _ = _softmax_pv(_dot_nt(q[:, c1], k_ref[:, c1]), vh)
        o2, _ = _softmax_pv(_dot_nt(q[:, c2], k_ref[:, c2]), vh)
        o = o1 - lam * o2
        o = o * lax.rsqrt(jnp.mean(o * o, axis=-1, keepdims=True) + EPS) * subln_ref[...] * (1.0 - lam_init)
        o_ref[:, h * DIFF_V_DIM:(h + 1) * DIFF_V_DIM] = o.astype(BF16)


def _diff_attention(q, k, v, lq1, lk1, lq2, lk2, subln, lam_init):
    n, s, c = q.shape
    tq = TQ_DENSE
    vec = lambda w: pl.BlockSpec((1, w), lambda b, i: (0, 0))
    blk = pl.BlockSpec((None, tq, c), lambda b, i: (b, i, 0))
    whole = pl.BlockSpec((None, s, c), lambda b, i: (b, 0, 0))
    return pl.pallas_call(
        functools.partial(_diff_kernel, lam_init=lam_init),
        out_shape=jax.ShapeDtypeStruct((n, s, c), BF16),
        grid=(n, s // tq),
        in_specs=[vec(HEAD_DIM)] * 4 + [vec(DIFF_V_DIM), blk, whole, whole],
        out_specs=blk,
        compiler_params=_params(("arbitrary", "arbitrary")),
        name="diff_attn",
    )(lq1, lk1, lq2, lk2, subln, q, k, v)


def _na_kernel(q_ref, k_ref, v_ref, b_ref, o_ref, *, rows):
    r = pl.program_id(1)
    rs = jnp.clip(r - NA_MAX_ROWS // 2, 0, rows - NA_MAX_ROWS)
    start = pl.multiple_of(rs * GRID_W, GRID_W)
    k = k_ref[pl.ds(start, NA_MAX_ROWS * GRID_W), :]
    v = v_ref[pl.ds(start, NA_MAX_ROWS * GRID_W), :]
    q = q_ref[...]
    outs = []
    for h in range(NA_HEADS):
        c = slice(h * HEAD_DIM, (h + 1) * HEAD_DIM)
        s = _dot_nt(q[:, c], k[:, c]) + b_ref[h, 0]
        o, _ = _softmax_pv(s, v[:, c])
        outs.append(o)
    o_ref[...] = jnp.concatenate(outs, axis=-1).astype(BF16)


def _na_attention(q, k, v, bias):
    n, s, c = q.shape
    rows = s // GRID_W
    assert rows >= NA_MAX_ROWS

    def bias_map(b, r):
        rs = jnp.clip(r - NA_MAX_ROWS // 2, 0, rows - NA_MAX_ROWS)
        return (0, r - rs, 0, 0)

    whole = pl.BlockSpec((None, s, c), lambda b, r: (b, 0, 0))
    blk = pl.BlockSpec((None, GRID_W, c), lambda b, r: (b, r, 0))
    return pl.pallas_call(
        functools.partial(_na_kernel, rows=rows),
        out_shape=jax.ShapeDtypeStruct((n, s, c), BF16),
        grid=(n, rows),
        in_specs=[blk, whole, whole,
                  pl.BlockSpec((NA_HEADS, 1, GRID_W, NA_MAX_ROWS * GRID_W), bias_map)],
        out_specs=blk,
        compiler_params=_params(("arbitrary", "arbitrary")),
        name="na_attn",
    )(q, k, v, bias)


def _mla_kernel(q_ref, k_ref, v_ref, o_ref):
    q = q_ref[...]
    outs = []
    for h in range(MLA_HEADS):
        c = slice(h * LANES, (h + 1) * LANES)
        o, _ = _softmax_pv(_dot_nt(q[:, c], k_ref[:, c]), v_ref[:, h * MLA_V:(h + 1) * MLA_V])
        outs.append(o)
    o_ref[...] = jnp.concatenate(outs, axis=-1).astype(BF16)


def _mla_attention(q, k, v):
    n, s, cq = q.shape
    cv = v.shape[-1]
    tq = TQ_DENSE
    return pl.pallas_call(
        _mla_kernel,
        out_shape=jax.ShapeDtypeStruct((n, s, cv), BF16),
        grid=(n, s // tq),
        in_specs=[pl.BlockSpec((None, tq, cq), lambda b, i: (b, i, 0)),
                  pl.BlockSpec((None, s, cq), lambda b, i: (b, 0, 0)),
                  pl.BlockSpec((None, s, cv), lambda b, i: (b, 0, 0))],
        out_specs=pl.BlockSpec((None, tq, cv), lambda b, i: (b, i, 0)),
        compiler_params=_params(("arbitrary", "arbitrary")),
        name="mla_attn",
    )(q, k, v)


def _even_out_kernel(x_ref, g_ref, o1_ref, o2_ref, o3_ref, l1_ref, l2_ref, l3_ref, ob_ref, w_ref, y_ref):
    l1, l2, l3 = l1_ref[...], l2_ref[...], l3_ref[...]
    m = jnp.maximum(jnp.maximum(l1, l2), l3)
    e1, e2, e3 = jnp.exp(l1 - m), jnp.exp(l2 - m), jnp.exp(l3 - m)
    oa = (e1 * o1_ref[...].astype(F32) + e2 * o2_ref[...].astype(F32) + e3 * o3_ref[...].astype(F32)) \
        * (1.0 / (e1 + e2 + e3))
    half = oa.shape[-1]
    mixed = _dot(oa.astype(BF16), w_ref[:half, :]) + _dot(ob_ref[...], w_ref[half:, :])
    y_ref[...] = x_ref[...] + g_ref[...] * mixed


def _even_out_proj(x, gate, o1, o2, o3, l1, l2, l3, ob, w):
    n, s, d = x.shape
    tm = TM_PROJ
    tok = pl.BlockSpec((None, tm, d), lambda b, i: (b, i, 0))
    half = pl.BlockSpec((None, tm, 512), lambda b, i: (b, i, 0))
    return pl.pallas_call(
        _even_out_kernel,
        out_shape=jax.ShapeDtypeStruct((n, s, d), F32),
        grid=(n, s // tm),
        in_specs=[tok, pl.BlockSpec((None, 1, d), lambda b, i: (b, 0, 0))] + [half] * 7
                 + [pl.BlockSpec(w.shape, lambda b, i: (0, 0))],
        out_specs=tok,
        compiler_params=_params(("arbitrary", "arbitrary")),
        name="even_out_proj",
    )(x, gate, o1, o2, o3, l1, l2, l3, ob, w)


def _odd_out_kernel(x_ref, g_ref, oc_ref, od_ref, w_ref, y_ref):
    half = oc_ref.shape[-1]
    mixed = _dot(oc_ref[...], w_ref[:half, :]) + _dot(od_ref[...], w_ref[half:, :])
    y_ref[...] = x_ref[...] + g_ref[...] * mixed


def _odd_out_proj(x, gate, oc, od, w):
    n, s, d = x.shape
    tm = TM_PROJ
    tok = pl.BlockSpec((None, tm, d), lambda b, i: (b, i, 0))
    half = pl.BlockSpec((None, tm, 512), lambda b, i: (b, i, 0))
    return pl.pallas_call(
        _odd_out_kernel,
        out_shape=jax.ShapeDtypeStruct((n, s, d), F32),
        grid=(n, s // tm),
        in_specs=[tok, pl.BlockSpec((None, 1, d), lambda b, i: (b, 0, 0)), half, half,
                  pl.BlockSpec(w.shape, lambda b, i: (0, 0))],
        out_specs=tok,
        compiler_params=_params(("arbitrary", "arbitrary")),
        name="odd_out_proj",
    )(x, gate, oc, od, w)


def _ffn_pre_kernel(x_ref, g_ref, sc_ref, sh_ref, wr_ref, rb_ref, wsg_ref, wsu_ref, wsd_ref,
                    h_ref, shared_ref, eidx_ref, ew_ref, cnt_ref):
    h = _norm_mod(x_ref[...], g_ref[...], sc_ref[...], sh_ref[...])
    h_ref[...] = h
    hb = h.astype(BF16)
    mid = _silu(_dot(hb, wsg_ref[...])) * _dot(hb, wsu_ref[...])
    shared_ref[...] = _dot(mid.astype(BF16), wsd_ref[...])

    logits = lax.dot_general(wr_ref[...], h, (((1,), (1,)), ((), ())),
                             precision=lax.Precision.HIGHEST, preferred_element_type=F32)
    scores = 1.0 / (1.0 + jnp.exp(-logits))
    biased = scores + rb_ref[...]
    tm = h.shape[0]

    blocks, gscore = [], []
    for g in range(N_GROUPS):
        blk = biased[g * GROUP_SIZE:(g + 1) * GROUP_SIZE, :]
        m1 = jnp.max(blk, axis=0, keepdims=True)
        is_max = blk == m1
        n_max = jnp.sum(is_max.astype(F32), axis=0, keepdims=True)
        m2 = jnp.max(jnp.where(is_max, -jnp.inf, blk), axis=0, keepdims=True)
        blocks.append(blk)
        gscore.append(m1 + jnp.where(n_max >= 2.0, m1, m2))
    masked = []
    for g in range(N_GROUPS):
        beaten = jnp.zeros((1, tm), F32)
        for o in range(N_GROUPS):
            if o == g:
                continue
            wins = (gscore[o] >= gscore[g]) if o < g else (gscore[o] > gscore[g])
            beaten = beaten + wins.astype(F32)
        masked.append(jnp.where(beaten < float(TOPK_GROUPS), blocks[g], -jnp.inf))
    cand = jnp.concatenate(masked, axis=0)

    eiota = lax.broadcasted_iota(jnp.int32, (N_EXPERTS, tm), 0)
    onehot = jnp.zeros((N_EXPERTS, tm), F32)
    idxs, ws = [], []
    for _ in range(TOP_K):
        m = jnp.max(cand, axis=0, keepdims=True)
        idx = jnp.min(jnp.where(cand == m, eiota, N_EXPERTS), axis=0, keepdims=True)
        pick = eiota == idx
        idxs.append(idx)
        ws.append(jnp.sum(jnp.where(pick, scores, 0.0), axis=0, keepdims=True))
        cand = jnp.where(pick, -jnp.inf, cand)
        onehot = onehot + pick.astype(F32)
    wsum = ws[0]
    for w in ws[1:]:
        wsum = wsum + w
    eidx_ref[...] = jnp.concatenate(idxs, axis=0)
    ew_ref[...] = jnp.concatenate(ws, axis=0) / wsum * ROUTED_SCALE

    first = (pl.program_id(0) == 0) & (pl.program_id(1) == 0)

    @pl.when(first)
    def _():
        cnt_ref[...] = jnp.zeros_like(cnt_ref)

    cnt_ref[...] += jnp.sum(onehot, axis=1, keepdims=True)


def _ffn_pre(x, gain, sc, sh, wr_t, rbias, wsg, wsu, wsd):
    n, s, d = x.shape
    tm = TM_ROUTE
    nt = s // tm
    t = n * s
    tok = pl.BlockSpec((None, tm, d), lambda b, i: (b, i, 0))
    per_b = pl.BlockSpec((None, 1, d), lambda b, i: (b, 0, 0))
    const = lambda shape: pl.BlockSpec(shape, lambda b, i: (0,) * len(shape))
    slot = pl.BlockSpec((TOP_K, tm), lambda b, i: (0, b * nt + i))
    return pl.pallas_call(
        _ffn_pre_kernel,
        out_shape=(jax.ShapeDtypeStruct((n, s, d), F32), jax.ShapeDtypeStruct((n, s, d), F32),
                   jax.ShapeDtypeStruct((TOP_K, t), jnp.int32), jax.ShapeDtypeStruct((TOP_K, t), F32),
                   jax.ShapeDtypeStruct((N_EXPERTS, 1), F32)),
        grid=(n, nt),
        in_specs=[tok, const((1, d)), per_b, per_b, const(wr_t.shape), const((N_EXPERTS, 1)),
                  const(wsg.shape), const(wsu.shape), const(wsd.shape)],
        out_specs=(tok, tok, slot, slot, const((N_EXPERTS, 1))),
        compiler_params=_params(("arbitrary", "arbitrary")),
        name="ffn_pre_router",
    )(x, gain, sc, sh, wr_t, rbias, wsg, wsu, wsd)


def _slot_kernel(eidx_ref, pstart_ref, dest_ref, carry_ref):
    @pl.when(pl.program_id(0) == 0)
    def _():
        carry_ref[...] = pstart_ref[...]

    tm = eidx_ref.shape[1]
    eiota = lax.broadcasted_iota(jnp.int32, (N_EXPERTS, tm), 0)
    picks = [eiota == eidx_ref[k:k + 1, :] for k in range(TOP_K)]
    onehot = picks[0].astype(F32)
    for p in picks[1:]:
        onehot = onehot + p.astype(F32)
    before = (lax.broadcasted_iota(jnp.int32, (tm, tm), 0) < lax.broadcasted_iota(jnp.int32, (tm, tm), 1))
    pos = carry_ref[...] + _dot(onehot.astype(BF16), before.astype(BF16))
    dest = [jnp.sum(jnp.where(p, pos, 0.0), axis=0, keepdims=True) for p in picks]
    dest_ref[...] = jnp.concatenate(dest, axis=0).astype(jnp.int32)
    carry_ref[...] += jnp.sum(onehot, axis=1, keepdims=True)


def _slots(eidx_t, pstart):
    t = eidx_t.shape[1]
    tm = TM_ROUTE
    blk = pl.BlockSpec((TOP_K, tm), lambda i: (0, i))
    return pl.pallas_call(
        _slot_kernel,
        out_shape=jax.ShapeDtypeStruct((TOP_K, t), jnp.int32),
        grid=(t // tm,),
        in_specs=[blk, pl.BlockSpec((N_EXPERTS, 1), lambda i: (0, 0))],
        out_specs=blk,
        scratch_shapes=[pltpu.VMEM((N_EXPERTS, 1), F32)],
        compiler_params=_params(("arbitrary",)),
        name="expert_slots",
    )(eidx_t, pstart)


def _dispatch_kernel(dest_ref, h_ref, xs_in_ref, xs_ref, sem):
    del xs_in_ref
    n = dest_ref.shape[0]

    def row_copy(i, d):
        return pltpu.make_async_copy(h_ref.at[pl.ds(i // TOP_K, 1)], xs_ref.at[pl.ds(d, 1)], sem)

    def issue(i, carry):
        row_copy(i, dest_ref[i]).start()
        return carry

    def drain(i, carry):
        row_copy(0, 0).wait()
        return carry

    lax.fori_loop(0, n, issue, 0)
    lax.fori_loop(0, n, drain, 0)


def _dispatch(dest_flat, h, n_rows):
    t, d = h.shape
    tm = TM_DISPATCH
    xs0 = jnp.zeros((n_rows, d), F32)
    return pl.pallas_call(
        _dispatch_kernel,
        out_shape=jax.ShapeDtypeStruct((n_rows, d), F32),
        grid=(t // tm,),
        in_specs=[pl.BlockSpec((tm * TOP_K,), lambda i: (i,), memory_space=pltpu.SMEM),
                  pl.BlockSpec((tm, d), lambda i: (i, 0)),
                  pl.BlockSpec(memory_space=pl.ANY)],
        out_specs=pl.BlockSpec(memory_space=pl.ANY),
        scratch_shapes=[pltpu.SemaphoreType.DMA],
        input_output_aliases={2: 0},
        compiler_params=_params(("arbitrary",)),
        name="moe_dispatch",
    )(dest_flat, h, xs0)


def _experts_kernel(be_ref, na_ref, xs_ref, wg_ref, wu_ref, wd_ref, o_ref):
    del be_ref

    @pl.when(pl.program_id(0) < na_ref[0])
    def _():
        x = xs_ref[...].astype(BF16)
        mid = _silu(_dot(x, wg_ref[...])) * _dot(x, wu_ref[...])
        o_ref[...] = _dot(mid.astype(BF16), wd_ref[...])


def _experts(block_e, n_active, xs, wg, wu, wd):
    p, d = xs.shape
    rows = EXPERT_ROWS
    nblk = p // rows
    live = lambda i, be, na: jnp.minimum(i, na[0] - 1)
    grid_spec = pltpu.PrefetchScalarGridSpec(
        num_scalar_prefetch=2,
        grid=(nblk,),
        in_specs=[pl.BlockSpec((rows, d), lambda i, be, na: (live(i, be, na), 0)),
                  pl.BlockSpec((None, d, MOE_D_FF), lambda i, be, na: (be[live(i, be, na)], 0, 0)),
                  pl.BlockSpec((None, d, MOE_D_FF), lambda i, be, na: (be[live(i, be, na)], 0, 0)),
                  pl.BlockSpec((None, MOE_D_FF, d), lambda i, be, na: (be[live(i, be, na)], 0, 0))],
        out_specs=pl.BlockSpec((rows, d), lambda i, be, na: (live(i, be, na), 0)),
    )
    return pl.pallas_call(
        _experts_kernel,
        out_shape=jax.ShapeDtypeStruct((p, d), F32),
        grid_spec=grid_spec,
        compiler_params=_params(("arbitrary",)),
        name="moe_experts",
    )(block_e, n_active, xs, wg, wu, wd)


def _combine_kernel(dest_ref, x_ref, g_ref, shared_ref, w_ref, ys_ref, y_ref, buf, sem):
    n = dest_ref.shape[0]

    def row_copy(i, d):
        return pltpu.make_async_copy(ys_ref.at[pl.ds(d, 1)], buf.at[i % TOP_K, pl.ds(i // TOP_K, 1)], sem)

    def issue(i, carry):
        row_copy(i, dest_ref[i]).start()
        return carry

    def drain(i, carry):
        row_copy(0, 0).wait()
        return carry

    lax.fori_loop(0, n, issue, 0)
    lax.fori_loop(0, n, drain, 0)
    w = w_ref[...]
    acc = shared_ref[...]
    for k in range(TOP_K):
        acc = acc + w[:, k:k + 1] * buf[k]
    y_ref[...] = x_ref[...] + g_ref[...] * acc


def _combine(dest_flat, x, gate, shared, w_tok, ys):
    n, s, d = x.shape
    tm = TM_COMBINE
    nt = s // tm
    tok = pl.BlockSpec((None, tm, d), lambda b, i: (b, i, 0))
    return pl.pallas_call(
        _combine_kernel,
        out_shape=jax.ShapeDtypeStruct((n, s, d), F32),
        grid=(n, nt),
        in_specs=[pl.BlockSpec((tm * TOP_K,), lambda b, i: (b * nt + i,), memory_space=pltpu.SMEM),
                  tok, pl.BlockSpec((None, 1, d), lambda b, i: (b, 0, 0)), tok,
                  pl.BlockSpec((tm, TOP_K), lambda b, i: (b * nt + i, 0)),
                  pl.BlockSpec(memory_space=pl.ANY)],
        out_specs=tok,
        scratch_shapes=[pltpu.VMEM((TOP_K, tm, d), F32), pltpu.SemaphoreType.DMA],
        compiler_params=_params(("arbitrary", "arbitrary")),
        name="moe_combine",
    )(dest_flat, x, gate, shared, w_tok, ys)


def _moe_layer(x, gain, sc, sh, gate, lw):
    n, s, d = x.shape
    t = n * s
    h, shared, eidx_t, ew_t, counts = _ffn_pre(x, gain, sc, sh, lw["router_t"], lw["router_bias"],
                                               lw["shared_gate"], lw["shared_up"], lw["shared_down"])
    rows = EXPERT_ROWS
    n_blocks = (t * TOP_K) // rows + N_EXPERTS
    cnt = counts[:, 0].astype(jnp.int32)
    padded = (cnt + rows - 1) // rows * rows
    pend = jnp.cumsum(padded)
    pstart = pend - padded
    n_active = (pend[-1:] // rows).astype(jnp.int32)
    block_e = jnp.minimum(
        jnp.searchsorted(pend, jnp.arange(n_blocks, dtype=jnp.int32) * rows, side="right"),
        N_EXPERTS - 1).astype(jnp.int32)
    dest_t = _slots(eidx_t, pstart.astype(F32)[:, None])
    dest_flat = dest_t.T.reshape(t * TOP_K)
    xs = _dispatch(dest_flat, h.reshape(t, d), n_blocks * rows)
    ys = _experts(block_e, n_active, xs, lw["expert_gate"], lw["expert_up"], lw["expert_down"])
    return _combine(dest_flat, x, gate, shared, ew_t.T, ys)


def _rope_tables(s, dim, lane_lo, heads_per_chunk):
    half = dim // 2
    inv = ROPE_THETA ** (-jnp.arange(half, dtype=F32) * 2.0 / dim)
    ang = jnp.arange(s, dtype=F32)[:, None] * inv[None, :]
    cos_h, sin_h = jnp.cos(ang), jnp.sin(ang)
    span = LANES // heads_per_chunk
    cos = jnp.ones((s, LANES), F32)
    sin = jnp.zeros((s, LANES), F32)
    prot = np.zeros((LANES, LANES), np.float32)
    for c in range(heads_per_chunk):
        lo = c * span + lane_lo
        cos = cos.at[:, lo:lo + dim].set(jnp.concatenate([cos_h, cos_h], axis=1))
        sin = sin.at[:, lo:lo + dim].set(jnp.concatenate([sin_h, sin_h], axis=1))
        for i in range(half):
            prot[lo + half + i, lo + i] = -1.0
            prot[lo + i, lo + half + i] = 1.0
    return cos, sin, jnp.asarray(prot, BF16)


def _head_sum_matrix():
    idx = np.arange(LANES) // HEAD_DIM
    return jnp.asarray((idx[:, None] == idx[None, :]).astype(np.float32), BF16)


def _tile_gain(g):
    return jnp.tile(g.astype(F32), LANES // g.shape[0])[None, :]


def _na_bias_table(rpb):
    kh, kw = NA_MAX_ROWS, NA_COLS
    cols = np.arange(GRID_W)
    col_start = np.clip(cols - kw // 2, 0, GRID_W - kw)
    col_ok = (cols[None, :] >= col_start[:, None]) & (cols[None, :] < col_start[:, None] + kw)
    dc = np.clip(cols[None, :] - cols[:, None], -(kw - 1), kw - 1) + kw - 1
    delta = np.arange(kh)
    dr = np.arange(kh)[None, :] - delta[:, None] + NA_MAX_ROWS - 1
    bias = rpb.astype(F32)[:, dr[:, None, :, None], dc[None, :, None, :]]
    bias = jnp.where(col_ok[None, None, :, None, :], bias, NEG)
    return bias.reshape(rpb.shape[0], kh, GRID_W, kh * GRID_W)


def _trunk(x, mods, p, tables):
    n, s, d = x.shape
    for i in range(DEPTH):
        sh_a, sc_a, g_a, sh_f, sc_f, g_f = [m[:, None, :] for m in jnp.split(mods[i], 6, axis=-1)]
        j = i // 2
        mix_gain = p["mix_norm"][i][None, :]
        if i % 2 == 0:
            e = p["even"][j]
            tb = tables["even"]
            qa, ka, va, qb, kb, vb = _even_in_proj(x, mix_gain, sc_a, sh_a, e["w_in"], tb["cos"], tb["sin"],
                                                   tables["bd"], tb["prot"], e["gqa"], e["gka"], e["gqb"], e["gkb"])
            outs, lses = [], []
            for window, dil in DIL_GROUPS:
                o, lse = _banded_attention(qa, ka, va, dil, (window // 2) // dil)
                outs.append(o)
                lses.append(lse)
            lam_init = 0.8 - 0.6 * math.exp(-0.3 * i)
            ob = _diff_attention(qb, kb, vb, e["lq1"], e["lk1"], e["lq2"], e["lk2"], e["subln"], lam_init)
            x = _even_out_proj(x, g_a, outs[0], outs[1], outs[2], lses[0], lses[1], lses[2], ob, e["w_out"])
        else:
            o_ = p["odd"][j]
            tb = tables["odd"]
            qc, kc, vc, qm, km, vm = _odd_in_proj(x, mix_gain, sc_a, sh_a, o_["w_in"], tb["cos"], tb["sin"],
                                                  tables["bd"], tb["prot"], o_["gqc"], o_["gkc"], o_["gcq"],
                                                  o_["gckv"], o_["wq"], o_["wk"], o_["wv"], o_["gq"], o_["gk"])
            oc = _na_attention(qc, kc, vc, o_["bias"])
            od = _mla_attention(qm, km, vm)
            x = _odd_out_proj(x, g_a, oc, od, o_["w_out"])
        x = _moe_layer(x, p["ffn_norm"][i][None, :], sc_f, sh_f, g_f, p["moe"][i])
    return x


def kernel(x_prompt, x_sample, c_prompt, c_sample, ada_w, ada_b, mix_norm, ffn_norm, even_w_in, even_w_out, dil_q_norm, dil_k_norm, diff_q_norm, diff_k_norm, diff_lambda_q1, diff_lambda_k1, diff_lambda_q2, diff_lambda_k2, diff_subln, odd_w_in, odd_w_out, na_q_norm, na_k_norm, na_rpb, mla_cq_norm, mla_ckv_norm, mla_w_q_up, mla_w_kv_up, mla_q_norm, mla_k_norm, router_w, router_bias, expert_w_gate, expert_w_up, expert_w_down, shared_w_gate, shared_w_up, shared_w_down):
    d = D_MODEL
    nb_p = c_prompt.shape[0]
    mods = _ada_mod(jnp.concatenate([c_prompt, c_sample], axis=0), ada_w, ada_b)

    even, odd, moe = [], [], []
    for j in range(even_w_in.shape[0]):
        even.append(dict(
            w_in=even_w_in[j].astype(BF16), w_out=even_w_out[j].astype(BF16),
            gqa=_tile_gain(dil_q_norm[j]), gka=_tile_gain(dil_k_norm[j]),
            gqb=_tile_gain(diff_q_norm[j]), gkb=_tile_gain(diff_k_norm[j]),
            lq1=diff_lambda_q1[j][None, :], lk1=diff_lambda_k1[j][None, :],
            lq2=diff_lambda_q2[j][None, :], lk2=diff_lambda_k2[j][None, :],
            subln=diff_subln[j][None, :]))
    for j in range(odd_w_in.shape[0]):
        w = odd_w_in[j]
        kr_cols = jnp.zeros((d, LANES), F32).at[:, MLA_NOPE:MLA_QK].set(w[:, 1920:1920 + MLA_ROPE])
        w_in = jnp.concatenate([w[:, :1920], kr_cols], axis=1).astype(BF16)
        wq = jnp.pad(mla_w_q_up[j].reshape(MLA_Q_LORA, MLA_HEADS, MLA_QK),
                     ((0, 0), (0, 0), (0, LANES - MLA_QK))).reshape(MLA_Q_LORA, MLA_HEADS * LANES)
        wkv = mla_w_kv_up[j].reshape(MLA_KV_LORA, MLA_HEADS, MLA_NOPE + MLA_V)
        wk = jnp.pad(wkv[:, :, :MLA_NOPE], ((0, 0), (0, 0), (0, LANES - MLA_NOPE))).reshape(MLA_KV_LORA, -1)
        wv = wkv[:, :, MLA_NOPE:].reshape(MLA_KV_LORA, MLA_HEADS * MLA_V)
        pad_gain = lambda g: jnp.pad(g.astype(F32), (0, LANES - MLA_QK))[None, :]
        odd.append(dict(
            w_in=w_in, w_out=odd_w_out[j].astype(BF16),
            gqc=_tile_gain(na_q_norm[j]), gkc=_tile_gain(na_k_norm[j]),
            gcq=mla_cq_norm[j][None, :], gckv=mla_ckv_norm[j][None, :],
            wq=wq.astype(BF16), wk=wk.astype(BF16), wv=wv.astype(BF16),
            gq=pad_gain(mla_q_norm[j]), gk=pad_gain(mla_k_norm[j]),
            bias=_na_bias_table(na_rpb[j])))
    for i in range(DEPTH):
        moe.append(dict(
            router_t=router_w[i].T, router_bias=router_bias[i][:, None],
            shared_gate=shared_w_gate[i].astype(BF16), shared_up=shared_w_up[i].astype(BF16),
            shared_down=shared_w_down[i].astype(BF16),
            expert_gate=expert_w_gate[i].astype(BF16), expert_up=expert_w_up[i].astype(BF16),
            expert_down=expert_w_down[i].astype(BF16)))
    params = dict(mix_norm=mix_norm, ffn_norm=ffn_norm, even=even, odd=odd, moe=moe)

    outs = []
    for x, lo, hi in ((x_prompt, 0, nb_p), (x_sample, nb_p, mods.shape[1])):
        s = x.shape[1]
        cos_e, sin_e, prot_e = _rope_tables(s, HEAD_DIM, 0, 2)
        cos_o, sin_o, prot_o = _rope_tables(s, MLA_ROPE, MLA_NOPE, 1)
        tables = dict(bd=_head_sum_matrix(),
                      even=dict(cos=cos_e, sin=sin_e, prot=prot_e),
                      odd=dict(cos=cos_o, sin=sin_o, prot=prot_o))
        outs.append(_trunk(x, mods[:, lo:hi], params, tables))
    return tuple(outs)
```
